```python
import math
import jax, jax.numpy as jnp
from jax import lax
import numpy as np


D_MODEL = 1024
BATCH = 16
SEQ = 2048
DEPTH = 1

ATTN_HEADS = 8
HEAD_DIM = 64
ATTN_WIDTH = ATTN_HEADS * HEAD_DIM
CONV_GROUPS = 8
CONV_CH = D_MODEL - ATTN_WIDTH
MIX_WIDTH = ATTN_WIDTH + CONV_CH
IN_COLS = 3 * ATTN_WIDTH + 2 * CONV_CH
CONV_KERNEL = 31
MOBA_BLOCK = 256
MOBA_TOPK = 3
Q_CHUNK = 128
NUM_BUCKETS = 32
MAX_DISTANCE = 128
D_FF = -(-8 * D_MODEL // (3 * 256)) * 256
RMS_EPS = 1e-6
LN_EPS = 1e-5
NEG_INF = -1e30
ATTN_SCALE = HEAD_DIM ** -0.5

kernel_name = 'hymba_moba_conformer_block'


def rms_norm(x, g):
    xf = x.astype(jnp.float32)
    y = xf * lax.rsqrt(jnp.mean(xf * xf, axis=-1, keepdims=True) + RMS_EPS)
    return (y * g.astype(jnp.float32)).astype(x.dtype)


def t5_bucket(dist):
    max_exact = NUM_BUCKETS // 2
    d = jnp.maximum(dist, 1).astype(jnp.float32)
    large = max_exact + (jnp.log(d / max_exact) / math.log(MAX_DISTANCE / max_exact)
                         * (NUM_BUCKETS - max_exact)).astype(jnp.int32)
    large = jnp.minimum(large, NUM_BUCKETS - 1)
    return jnp.where(dist < max_exact, dist, large)


def moba_attention(q, k, v, rel_bias):
    B, H, S, Dh = q.shape
    nb = -(-S // MOBA_BLOCK)
    s_pad = nb * MOBA_BLOCK
    n_sel = max(1, min(MOBA_TOPK, nb - 1))
    nq = S // Q_CHUNK
    pad = ((0, 0), (0, 0), (0, s_pad - S), (0, 0))
    k_blocks = jnp.pad(k, pad).reshape(B, H, nb, MOBA_BLOCK, Dh)
    v_blocks = jnp.pad(v, pad).reshape(B, H, nb, MOBA_BLOCK, Dh)

    k_mean = jnp.mean(k_blocks.astype(jnp.float32), axis=3)
    gate = jnp.einsum('bhsd,bhnd->bhsn', q.astype(jnp.float32), k_mean)
    q_block = jnp.arange(S, dtype=jnp.int32) // MOBA_BLOCK
    fully_past = jnp.arange(nb, dtype=jnp.int32)[None, :] < q_block[:, None]
    gate = jnp.where(fully_past, gate, NEG_INF)
    _, sel = lax.top_k(gate, n_sel)
    sel = sel.astype(jnp.int32)

    bias_ht = rel_bias.T.astype(jnp.float32)
    head_ix = jnp.arange(H)
    offs = jnp.arange(MOBA_BLOCK, dtype=jnp.int32)
    rank_ix = jnp.arange(n_sel, dtype=jnp.int32)

    q_c = q.reshape(B, H, nq, Q_CHUNK, Dh).transpose(0, 2, 1, 3, 4).reshape(B * nq, H, Q_CHUNK, Dh)
    sel_c = sel.reshape(B, H, nq, Q_CHUNK, n_sel).transpose(0, 2, 1, 3, 4).reshape(B * nq, H, Q_CHUNK, n_sel)
    b_ids = jnp.repeat(jnp.arange(B, dtype=jnp.int32), nq)
    c_ids = jnp.tile(jnp.arange(nq, dtype=jnp.int32), B)

    def chunk(args):
        qc, sc, b, c = args
        kb = k_blocks[b]
        vb = v_blocks[b]
        t = c * Q_CHUNK + jnp.arange(Q_CHUNK, dtype=jnp.int32)
        own = (c * Q_CHUNK) // MOBA_BLOCK
        k_sel = kb[head_ix[:, None, None], sc]
        v_sel = vb[head_ix[:, None, None], sc]
        s_sel = jnp.einsum('hqd,hqnkd->hqnk', qc, k_sel).astype(jnp.float32) * ATTN_SCALE
        dist_sel = t[None, :, None, None] - (sc[..., None] * MOBA_BLOCK + offs)
        s_sel = s_sel + bias_ht[head_ix[:, None, None, None], t5_bucket(jnp.maximum(dist_sel, 0))]
        valid = (rank_ix < own)[None, None, :, None]
        s_sel = jnp.where(valid, s_sel, NEG_INF)
        k_own = kb[:, own]
        v_own = vb[:, own]
        s_own = jnp.einsum('hqd,hkd->hqk', qc, k_own).astype(jnp.float32) * ATTN_SCALE
        dist_own = t[:, None] - (own * MOBA_BLOCK + offs)[None, :]
        s_own = s_own + bias_ht[:, t5_bucket(jnp.maximum(dist_own, 0))]
        s_own = jnp.where((dist_own >= 0)[None], s_own, NEG_INF)
        logits = jnp.concatenate([s_sel.reshape(H, Q_CHUNK, n_sel * MOBA_BLOCK), s_own], axis=-1)
        p = jax.nn.softmax(logits, axis=-1).astype(v.dtype)
        p_sel = p[..., :n_sel * MOBA_BLOCK].reshape(H, Q_CHUNK, n_sel, MOBA_BLOCK)
        p_own = p[..., n_sel * MOBA_BLOCK:]
        return (jnp.einsum('hqnk,hqnkd->hqd', p_sel, v_sel)
                + jnp.einsum('hqk,hkd->hqd', p_own, v_own))

    out = lax.map(chunk, (q_c, sel_c, b_ids, c_ids))
    return out.reshape(B, nq, H, Q_CHUNK, Dh).transpose(0, 1, 3, 2, 4).reshape(B, S, H * Dh)


def conformer_conv(a, g, w_dw, b_dw, ln_g, ln_b):
    C = a.shape[-1]
    u = a * jax.nn.sigmoid(g)
    u = lax.conv_general_dilated(u, w_dw[:, None, :], window_strides=(1,),
                                 padding=((CONV_KERNEL - 1, 0),),
                                 dimension_numbers=('NWC', 'WIO', 'NWC'),
                                 feature_group_count=C) + b_dw
    uf = u.astype(jnp.float32)
    mu = jnp.mean(uf, axis=-1, keepdims=True)
    var = jnp.mean(jnp.square(uf - mu), axis=-1, keepdims=True)
    un = (uf - mu) * lax.rsqrt(var + LN_EPS) * ln_g.astype(jnp.float32) + ln_b.astype(jnp.float32)
    return jax.nn.silu(un).astype(a.dtype)


def setup_inputs(seed: int = 0) -> dict:
    key = jax.random.key(seed)
    ks = jax.random.split(key, 16)
    f32 = jnp.float32
    nrm = lambda k, shape, scale: jax.random.normal(k, shape, f32) * scale
    return {
        'x': jax.random.normal(ks[0], (BATCH, SEQ, D_MODEL), f32),
        'mix_norm_g': 1.0 + nrm(ks[1], (DEPTH, D_MODEL), 0.02),
        'w_in': nrm(ks[2], (DEPTH, D_MODEL, IN_COLS), D_MODEL ** -0.5),
        'rel_bias': nrm(ks[3], (NUM_BUCKETS, ATTN_HEADS), 0.5),
        'conv_w': nrm(ks[4], (DEPTH, CONV_KERNEL, CONV_CH), CONV_KERNEL ** -0.5),
        'conv_b': nrm(ks[5], (DEPTH, CONV_CH), 0.02),
        'conv_ln_g': 1.0 + nrm(ks[6], (DEPTH, CONV_CH), 0.02),
        'conv_ln_b': nrm(ks[7], (DEPTH, CONV_CH), 0.02),
        'w_out': nrm(ks[8], (DEPTH, MIX_WIDTH, D_MODEL), MIX_WIDTH ** -0.5),
        'ffn_norm_g': 1.0 + nrm(ks[9], (DEPTH, D_MODEL), 0.02),
        'w_gate': nrm(ks[10], (DEPTH, D_MODEL, D_FF), D_MODEL ** -0.5),
        'w_up': nrm(ks[11], (DEPTH, D_MODEL, D_FF), D_MODEL ** -0.5),
        'w_down': nrm(ks[12], (DEPTH, D_FF, D_MODEL), D_FF ** -0.5),
        'final_norm_g': 1.0 + nrm(ks[13], (D_MODEL,), 0.02),
    }


def reference(x, mix_norm_g, w_in, rel_bias, conv_w, conv_b, conv_ln_g, conv_ln_b,
              w_out, ffn_norm_g, w_gate, w_up, w_down, final_norm_g):
    B, S, _ = x.shape
    splits = [ATTN_WIDTH, 2 * ATTN_WIDTH, 3 * ATTN_WIDTH, 3 * ATTN_WIDTH + CONV_CH]

    def heads(t):
        return t.reshape(B, S, ATTN_HEADS, HEAD_DIM).transpose(0, 2, 1, 3)

    h = x
    for l in range(DEPTH):
        u = rms_norm(h, mix_norm_g[l])
        proj = u @ w_in[l]
        q, k, v, glu_a, glu_g = jnp.split(proj, splits, axis=-1)
        attn_out = moba_attention(heads(q), heads(k), heads(v), rel_bias)
        conv_out = conformer_conv(glu_a, glu_g, conv_w[l], conv_b[l],
                                  conv_ln_g[l], conv_ln_b[l])
        mixed = jnp.concatenate([attn_out, conv_out], axis=-1)
        h = h + mixed @ w_out[l]
        u = rms_norm(h, ffn_norm_g[l])
        h = h + (jax.nn.silu(u @ w_gate[l]) * (u @ w_up[l])) @ w_down[l]
    return rms_norm(h, final_norm_g)
```

```python
import functools
import math

import jax
import jax.numpy as jnp
import numpy as np
from jax import lax
from jax.experimental import pallas as pl
from jax.experimental.pallas import tpu as pltpu

ATTN_HEADS = 8
HEAD_DIM = 64
ATTN_WIDTH = ATTN_HEADS * HEAD_DIM
CONV_KERNEL = 31
MOBA_BLOCK = 256
MOBA_TOPK = 3
NUM_BUCKETS = 32
MAX_DISTANCE = 128
RMS_EPS = 1e-6
LN_EPS = 1e-5
NEG_INF = -1e30
ATTN_SCALE = HEAD_DIM ** -0.5

LANES = 128
HEADS_PER_SLAB = LANES // HEAD_DIM
ROW_TILE = 512
CONV_ROWS = 512
CONV_HALO = 32
CONV_SUB = 64
FFN_CHUNK = 256
VMEM_LIMIT = 56 * 1024 * 1024

F32 = jnp.float32
BF16 = jnp.bfloat16


def _dot(a, b):
    return jnp.dot(a, b, preferred_element_type=F32)


def _dot_nt(a, b):
    return lax.dot_general(a, b, (((1,), (1,)), ((), ())), preferred_element_type=F32)


def _rms(x, g):
    return x * lax.rsqrt(jnp.mean(x * x, axis=-1, keepdims=True) + RMS_EPS) * g


def _inproj_kernel(x_ref, g_ref, w_ref, wvt_ref, q_ref, k_ref, vt_ref, u_ref, km_ref):
    y = _rms(x_ref[...], g_ref[...]).astype(BF16)
    aw = ATTN_WIDTH
    q_ref[...] = (_dot(y, w_ref[:, 0:aw]) * ATTN_SCALE).astype(BF16)
    k = _dot(y, w_ref[:, aw:2 * aw])
    k_ref[...] = k.astype(BF16)
    nblk = k.shape[0] // MOBA_BLOCK
    km_ref[...] = jnp.mean(k.reshape(nblk, MOBA_BLOCK, aw), axis=1)
    vt = _dot_nt(wvt_ref[...], y)
    for n in range(nblk):
        vt_ref[n] = vt[:, n * MOBA_BLOCK:(n + 1) * MOBA_BLOCK].astype(BF16)
    cw = u_ref.shape[1]
    a = _dot(y, w_ref[:, 3 * aw:3 * aw + cw])
    gate = _dot(y, w_ref[:, 3 * aw + cw:3 * aw + 2 * cw])
    u_ref[...] = (a * jax.nn.sigmoid(gate)).astype(BF16)


def _inproj(x2, norm_g, w_in_bf, wvt_bf, conv_ch, batch, seq):
    n, d = x2.shape
    steps = n // ROW_TILE
    tiles_per_seq = seq // ROW_TILE
    blk_per_tile = ROW_TILE // MOBA_BLOCK
    nblk = seq // MOBA_BLOCK
    const = lambda r: (0, 0)
    return pl.pallas_call(
        _inproj_kernel,
        grid=(steps,),
        in_specs=[
            pl.BlockSpec((ROW_TILE, d), lambda r: (r, 0)),
            pl.BlockSpec((1, d), const),
            pl.BlockSpec(w_in_bf.shape, const),
            pl.BlockSpec(wvt_bf.shape, const),
        ],
        out_specs=[
            pl.BlockSpec((ROW_TILE, ATTN_WIDTH), lambda r: (r, 0)),
            pl.BlockSpec((ROW_TILE, ATTN_WIDTH), lambda r: (r, 0)),
            pl.BlockSpec((None, blk_per_tile, ATTN_WIDTH, MOBA_BLOCK),
                         lambda r: (r // tiles_per_seq, r % tiles_per_seq, 0, 0)),
            pl.BlockSpec((ROW_TILE, conv_ch), lambda r: (r, 0)),
            pl.BlockSpec((None, blk_per_tile, ATTN_WIDTH), lambda r: (r, 0, 0)),
        ],
        out_shape=[
            jax.ShapeDtypeStruct((n, ATTN_WIDTH), BF16),
            jax.ShapeDtypeStruct((n, ATTN_WIDTH), BF16),
            jax.ShapeDtypeStruct((batch, nblk, ATTN_WIDTH, MOBA_BLOCK), BF16),
            jax.ShapeDtypeStruct((n, conv_ch), BF16),
            jax.ShapeDtypeStruct((steps, blk_per_tile, ATTN_WIDTH), F32),
        ],
        compiler_params=pltpu.CompilerParams(
            dimension_semantics=("parallel",), vmem_limit_bytes=VMEM_LIMIT),
        name="inproj",
    )(x2, norm_g, w_in_bf, wvt_bf)


def _attn_kernel(q_ref, k_ref, vt_ref, km_ref, bown_ref, bpast_ref, o_ref, mask_ref):
    i = pl.program_id(1)
    nblk = k_ref.shape[0]
    tq = q_ref.shape[0]
    n_sel = jnp.minimum(i, MOBA_TOPK)
    blk_row = lax.broadcasted_iota(jnp.int32, (nblk, tq), 0)
    lane = lax.broadcasted_iota(jnp.int32, (tq, LANES), 1)

    for p in range(ATTN_HEADS // HEADS_PER_SLAB):
        slab = slice(p * LANES, (p + 1) * LANES)
        q_slab = q_ref[:, slab]
        km_slab = km_ref[:, slab]
        km_hi = km_slab.astype(BF16).astype(F32)
        km_lo = (km_slab - km_hi).astype(BF16).astype(F32)
        out_t = []
        for hh in range(HEADS_PER_SLAB):
            h = p * HEADS_PER_SLAB + hh
            rows = slice(h * HEAD_DIM, (h + 1) * HEAD_DIM)
            in_head = (lane >= hh * HEAD_DIM) & (lane < (hh + 1) * HEAD_DIM)
            qm = jnp.where(in_head, q_slab, jnp.zeros_like(q_slab))

            qf = qm.astype(F32)
            gate = _dot_nt(km_hi, qf) + _dot_nt(km_lo, qf)
            gate = jnp.where(blk_row < i, gate, NEG_INF)
            rank = jnp.zeros((nblk, tq), jnp.int32)
            for r in range(nblk):
                g_r = gate[r:r + 1, :]
                beats = (g_r > gate) | ((g_r == gate) & (blk_row > r))
                rank = rank + beats.astype(jnp.int32)
            sel = (rank < n_sel) & (blk_row < i)
            mask_t = jnp.where(sel, 0.0, NEG_INF).astype(F32)
            for r in range(nblk):
                mask_ref[h, r] = jnp.broadcast_to(mask_t[r:r + 1, :], (8, tq))

            s_t = _dot_nt(k_ref[i, :, slab], qm) + bown_ref[h]
            m = jnp.max(s_t, axis=0, keepdims=True)
            p_t = jnp.exp(s_t - m)
            l = jnp.sum(p_t, axis=0, keepdims=True)
            acc = _dot(vt_ref[i, rows, :], p_t.astype(BF16))

            def past_block(j, carry, h=h, rows=rows, slab=slab, qm=qm):
                m, l, acc = carry
                near = jnp.minimum(i - 1 - j, 1)
                s_t = (_dot_nt(k_ref[j, :, slab], qm) + bpast_ref[h, near]
                       + mask_ref[h, j][0:1, :])
                m_new = jnp.maximum(m, jnp.max(s_t, axis=0, keepdims=True))
                alpha = jnp.exp(m - m_new)
                p_t = jnp.exp(s_t - m_new)
                l = alpha * l + jnp.sum(p_t, axis=0, keepdims=True)
                acc = alpha * acc + _dot(vt_ref[j, rows, :], p_t.astype(BF16))
                return m_new, l, acc

            m, l, acc = lax.fori_loop(0, i, past_block, (m, l, acc))
            out_t.append(acc * (1.0 / l))
        pair_t = jnp.concatenate(out_t, axis=0)
        o_ref[:, slab] = pair_t.T.astype(o_ref.dtype)


def _attention(q, k4, vt4, km3, bown, bpast):
    n = q.shape[0]
    batch, nblk = k4.shape[0], k4.shape[1]
    tq = MOBA_BLOCK
    return pl.pallas_call(
        _attn_kernel,
        grid=(batch, nblk),
        in_specs=[
            pl.BlockSpec((tq, ATTN_WIDTH), lambda b, i: (b * nblk + i, 0)),
            pl.BlockSpec((None,) + k4.shape[1:], lambda b, i: (b, 0, 0, 0)),
            pl.BlockSpec((None,) + vt4.shape[1:], lambda b, i: (b, 0, 0, 0)),
            pl.BlockSpec((None,) + km3.shape[1:], lambda b, i: (b, 0, 0)),
            pl.BlockSpec(bown.shape, lambda b, i: (0, 0, 0)),
            pl.BlockSpec(bpast.shape, lambda b, i: (0, 0, 0, 0)),
        ],
        out_specs=pl.BlockSpec((tq, ATTN_WIDTH), lambda b, i: (b * nblk + i, 0)),
        out_shape=jax.ShapeDtypeStruct((n, ATTN_WIDTH), BF16),
        scratch_shapes=[pltpu.VMEM((ATTN_HEADS, nblk, 8, tq), F32)],
        compiler_params=pltpu.CompilerParams(
            dimension_semantics=("parallel", "arbitrary"), vmem_limit_bytes=VMEM_LIMIT),
        name="moba_attn",
    )(q, k4, vt4, km3, bown, bpast)


def _t5_bucket_table(max_dist):
    d = np.arange(max_dist, dtype=np.int64)
    max_exact = NUM_BUCKETS // 2
    df = np.maximum(d, 1).astype(np.float32)
    large = max_exact + (np.log(df / np.float32(max_exact)) / np.float32(math.log(MAX_DISTANCE / max_exact))
                         * np.float32(NUM_BUCKETS - max_exact)).astype(np.int32)
    large = np.minimum(large, NUM_BUCKETS - 1)
    return np.where(d < max_exact, d, large).astype(np.int32)


def _bias_tiles(rel_bias):
    blk = MOBA_BLOCK
    by_dist = rel_bias.astype(F32).T[:, _t5_bucket_table(2 * blk)]
    kj = np.arange(blk)[:, None]
    qi = np.arange(blk)[None, :]
    own = jnp.where(jnp.asarray(qi >= kj)[None], by_dist[:, np.maximum(qi - kj, 0)], NEG_INF)
    prev = by_dist[:, blk + qi - kj]
    far = jnp.broadcast_to(by_dist[:, 2 * blk - 1][:, None, None], prev.shape)
    return own, jnp.stack([prev, far], axis=1)


def _conv_kernel(prev_ref, cur_ref, w_ref, b_ref, g_ref, beta_ref, o_ref, win_ref):
    c = pl.program_id(1)
    halo = prev_ref[...].astype(F32)
    win_ref[0:CONV_HALO, :] = jnp.where(c > 0, halo, 0.0)
    win_ref[CONV_HALO:, :] = cur_ref[...].astype(F32)
    rows, ch = cur_ref.shape
    for t in range(rows // CONV_SUB):
        acc = jnp.broadcast_to(b_ref[...], (CONV_SUB, ch))
        for j in range(CONV_KERNEL):
            off = CONV_HALO + t * CONV_SUB - (CONV_KERNEL - 1) + j
            acc = acc + win_ref[off:off + CONV_SUB, :] * w_ref[j:j + 1, :]
        mu = jnp.mean(acc, axis=-1, keepdims=True)
        dev = acc - mu
        var = jnp.mean(dev * dev, axis=-1, keepdims=True)
        y = dev * lax.rsqrt(var + LN_EPS) * g_ref[...] + beta_ref[...]
        o_ref[t * CONV_SUB:(t + 1) * CONV_SUB, :] = (y * jax.nn.sigmoid(y)).astype(o_ref.dtype)


def _conformer_conv(u, conv_w, conv_b, ln_g, ln_b, batch, seq):
    n, ch = u.shape
    tiles = seq // CONV_ROWS
    halo_per_tile = CONV_ROWS // CONV_HALO
    halo_per_seq = seq // CONV_HALO
    const = lambda b, c: (0, 0)
    return pl.pallas_call(
        _conv_kernel,
        grid=(batch, tiles),
        in_specs=[
            pl.BlockSpec((CONV_HALO, ch),
                         lambda b, c: (b * halo_per_seq + jnp.maximum(c * halo_per_tile - 1, 0), 0)),
            pl.BlockSpec((CONV_ROWS, ch), lambda b, c: (b * tiles + c, 0)),
            pl.BlockSpec(conv_w.shape, const),
            pl.BlockSpec((1, ch), const),
            pl.BlockSpec((1, ch), const),
            pl.BlockSpec((1, ch), const),
        ],
        out_specs=pl.BlockSpec((CONV_ROWS, ch), lambda b, c: (b * tiles + c, 0)),
        out_shape=jax.ShapeDtypeStruct((n, ch), BF16),
        scratch_shapes=[pltpu.VMEM((CONV_HALO + CONV_ROWS, ch), F32)],
        compiler_params=pltpu.CompilerParams(
            dimension_semantics=("parallel", "arbitrary"), vmem_limit_bytes=VMEM_LIMIT),
        name="conformer_conv",
    )(u, u, conv_w, conv_b, ln_g, ln_b)


def _ffn_kernel(x_ref, attn_ref, conv_ref, wo_ref, g2_ref, wg_ref, wu_ref, wd_ref, gf_ref,
                o_ref, acc_ref, *, apply_final_norm):
    aw = attn_ref.shape[1]
    h1 = x_ref[...] + _dot(attn_ref[...], wo_ref[0:aw, :]) + _dot(conv_ref[...], wo_ref[aw:, :])
    u2 = _rms(h1, g2_ref[...]).astype(BF16)
    acc_ref[...] = jnp.zeros_like(acc_ref)

    def chunk(c, carry):
        gate = _dot(u2, wg_ref[c])
        up = _dot(u2, wu_ref[c])
        act = (gate * jax.nn.sigmoid(gate) * up).astype(BF16)
        acc_ref[...] += _dot(act, wd_ref[c])
        return carry

    lax.fori_loop(0, wg_ref.shape[0], chunk, 0)
    h2 = h1 + acc_ref[...]
    if apply_final_norm:
        h2 = _rms(h2, gf_ref[...])
    o_ref[...] = h2


def _outproj_ffn(x2, attn, conv, wo_bf, g2, wg3, wu3, wd3, gf, apply_final_norm):
    n, d = x2.shape
    resident = pl.Buffered(1)
    row = lambda r: (r, 0)
    const2 = lambda r: (0, 0)
    const3 = lambda r: (0, 0, 0)
    return pl.pallas_call(
        functools.partial(_ffn_kernel, apply_final_norm=apply_final_norm),
        grid=(n // ROW_TILE,),
        in_specs=[
            pl.BlockSpec((ROW_TILE, d), row),
            pl.BlockSpec((ROW_TILE, attn.shape[1]), row),
            pl.BlockSpec((ROW_TILE, conv.shape[1]), row),
            pl.BlockSpec(wo_bf.shape, const2, pipeline_mode=resident),
            pl.BlockSpec((1, d), const2),
            pl.BlockSpec(wg3.shape, const3, pipeline_mode=resident),
            pl.BlockSpec(wu3.shape, const3, pipeline_mode=resident),
            pl.BlockSpec(wd3.shape, const3, pipeline_mode=resident),
            pl.BlockSpec((1, d), const2),
        ],
        out_specs=pl.BlockSpec((ROW_TILE, d), row),
        out_shape=jax.ShapeDtypeStruct((n, d), F32),
        scratch_shapes=[pltpu.VMEM((ROW_TILE, d), F32)],
        compiler_params=pltpu.CompilerParams(
            dimension_semantics=("parallel",), vmem_limit_bytes=VMEM_LIMIT),
        name="outproj_ffn",
    )(x2, attn, conv, wo_bf, g2, wg3, wu3, wd3, gf)


def kernel(x, mix_norm_g, w_in, rel_bias, conv_w, conv_b, conv_ln_g, conv_ln_b, w_out, ffn_norm_g,
           w_gate, w_up, w_down, final_norm_g):
    batch, seq, d = x.shape
    depth = w_in.shape[0]
    conv_ch = conv_w.shape[-1]
    d_ff = w_gate.shape[-1]
    nblk = seq // MOBA_BLOCK
    n_chunks = d_ff // FFN_CHUNK
    assert seq % ROW_TILE == 0 and seq % CONV_ROWS == 0 and d_ff % FFN_CHUNK == 0
    assert w_in.shape[-1] == 3 * ATTN_WIDTH + 2 * conv_ch

    bown, bpast = _bias_tiles(rel_bias)
    h = x.reshape(batch * seq, d)
    for l in range(depth):
        w_in_bf = w_in[l].astype(BF16)
        wvt_bf = w_in[l][:, 2 * ATTN_WIDTH:3 * ATTN_WIDTH].T.astype(BF16)
        q, k, vt4, u, km = _inproj(h, mix_norm_g[l][None], w_in_bf, wvt_bf, conv_ch, batch, seq)
        attn = _attention(q, k.reshape(batch, nblk, MOBA_BLOCK, ATTN_WIDTH), vt4,
                          km.reshape(batch, nblk, ATTN_WIDTH), bown, bpast)
        conv = _conformer_conv(u, conv_w[l], conv_b[l][None], conv_ln_g[l][None], conv_ln_b[l][None],
                               batch, seq)
        wg3 = w_gate[l].astype(BF16).reshape(d, n_chunks, FFN_CHUNK).transpose(1, 0, 2)
        wu3 = w_up[l].astype(BF16).reshape(d, n_chunks, FFN_CHUNK).transpose(1, 0, 2)
        wd3 = w_down[l].astype(BF16).reshape(n_chunks, FFN_CHUNK, d)
        h = _outproj_ffn(h, attn, conv, w_out[l].astype(BF16), ffn_norm_g[l][None], wg3, wu3, wd3,
                         final_norm_g[None], apply_final_norm=(l == depth - 1))
    return h.reshape(batch, seq, d)
```

```python
import functools
import math

import jax
import jax.numpy as jnp
import numpy as np
from jax import lax
from jax.experimental import pallas as pl
from jax.experimental.pallas import tpu as pltpu

ATTN_HEADS = 8
HEAD_DIM = 64
ATTN_WIDTH = ATTN_HEADS * HEAD_DIM
CONV_KERNEL = 31
MOBA_BLOCK = 256
MOBA_TOPK = 3
NUM_BUCKETS = 32
MAX_DISTANCE = 128
RMS_EPS = 1e-6
LN_EPS = 1e-5
NEG_INF = -1e30
ATTN_SCALE = HEAD_DIM ** -0.5

LANES = 128
SUBLANES = 8
HEADS_PER_SLAB = LANES // HEAD_DIM
ROW_TILE = 512
CONV_ROWS = 512
CONV_HALO = 32
CONV_SUB = 64
FFN_CHUNK = 256
QK_LOOKAHEAD = 4
KEY_CHUNK = 128
VMEM_LIMIT = 56 * 1024 * 1024

F32 = jnp.float32
BF16 = jnp.bfloat16


def _dot(a, b):
    return jnp.dot(a, b, preferred_element_type=F32)


def _dot_nt(a, b):
    return lax.dot_general(a, b, (((1,), (1,)), ((), ())), preferred_element_type=F32)


def _rms(x, g):
    return x * lax.rsqrt(jnp.mean(x * x, axis=-1, keepdims=True) + RMS_EPS) * g


def _t5_bucket_table(max_dist):
    d = np.arange(max_dist, dtype=np.int64)
    max_exact = NUM_BUCKETS // 2
    df = np.maximum(d, 1).astype(np.float32)
    large = max_exact + (np.log(df / np.float32(max_exact)) / np.float32(math.log(MAX_DISTANCE / max_exact))
                         * np.float32(NUM_BUCKETS - max_exact)).astype(np.int32)
    large = np.minimum(large, NUM_BUCKETS - 1)
    return np.where(d < max_exact, d, large).astype(np.int32)


def _bias_tile_kernel(bd_ref, own_ref, prev_ref):
    blk = own_ref.shape[0]
    by_dist = bd_ref[...]

    def toeplitz(w):
        x = jnp.broadcast_to(w, (blk, 2 * blk))
        return pltpu.roll(x, 0, 1, stride=1, stride_axis=0)[:, blk:]

    prev_ref[...] = toeplitz(by_dist)
    causal = jnp.concatenate([jnp.full((1, blk), NEG_INF, F32), by_dist[:, :blk]], axis=1)
    own_ref[...] = toeplitz(causal)


def _bias_tiles(rel_bias):
    blk = MOBA_BLOCK
    heads = rel_bias.shape[1]
    by_dist = rel_bias.astype(F32).T[:, _t5_bucket_table(2 * blk)]
    tile = jax.ShapeDtypeStruct((heads, blk, blk), F32)
    own, prev = pl.pallas_call(
        _bias_tile_kernel,
        grid=(heads,),
        in_specs=[pl.BlockSpec((None, 1, 2 * blk), lambda h: (h, 0, 0))],
        out_specs=[pl.BlockSpec((None, blk, blk), lambda h: (h, 0, 0))] * 2,
        out_shape=[tile, tile],
        compiler_params=pltpu.CompilerParams(dimension_semantics=("parallel",)),
        name="t5_bias_tiles",
    )(by_dist[:, None, :])
    return own, prev, by_dist[:, 2 * blk - 1]


def _inproj_kernel(x_ref, g_ref, w_ref, wvt_ref, q_ref, k_ref, vt_ref, u_ref, km_ref):
    y = _rms(x_ref[...], g_ref[...]).astype(BF16)
    aw = ATTN_WIDTH
    q_ref[...] = (_dot(y, w_ref[:, 0:aw]) * ATTN_SCALE).astype(BF16)
    k = _dot(y, w_ref[:, aw:2 * aw])
    k_ref[...] = k.astype(BF16)
    nblk = k.shape[0] // MOBA_BLOCK
    km_ref[...] = jnp.mean(k.reshape(nblk, MOBA_BLOCK, aw), axis=1)
    vt = _dot_nt(wvt_ref[...], y)
    for n in range(nblk):
        vt_ref[n] = vt[:, n * MOBA_BLOCK:(n + 1) * MOBA_BLOCK].astype(BF16)
    cw = u_ref.shape[1]
    a = _dot(y, w_ref[:, 3 * aw:3 * aw + cw])
    gate = _dot(y, w_ref[:, 3 * aw + cw:3 * aw + 2 * cw])
    u_ref[...] = (a * jax.nn.sigmoid(gate)).astype(BF16)


def _inproj(x2, norm_g, w_in_bf, wvt_bf, conv_ch, batch, seq):
    n, d = x2.shape
    steps = n // ROW_TILE
    tiles_per_seq = seq // ROW_TILE
    blk_per_tile = ROW_TILE // MOBA_BLOCK
    nblk = seq // MOBA_BLOCK
    const = lambda r: (0, 0)
    return pl.pallas_call(
        _inproj_kernel,
        grid=(steps,),
        in_specs=[
            pl.BlockSpec((ROW_TILE, d), lambda r: (r, 0)),
            pl.BlockSpec((1, d), const),
            pl.BlockSpec(w_in_bf.shape, const),
            pl.BlockSpec(wvt_bf.shape, const),
        ],
        out_specs=[
            pl.BlockSpec((ROW_TILE, ATTN_WIDTH), lambda r: (r, 0)),
            pl.BlockSpec((ROW_TILE, ATTN_WIDTH), lambda r: (r, 0)),
            pl.BlockSpec((None, blk_per_tile, ATTN_WIDTH, MOBA_BLOCK),
                         lambda r: (r // tiles_per_seq, r % tiles_per_seq, 0, 0)),
            pl.BlockSpec((ROW_TILE, conv_ch), lambda r: (r, 0)),
            pl.BlockSpec((None, blk_per_tile, ATTN_WIDTH), lambda r: (r, 0, 0)),
        ],
        out_shape=[
            jax.ShapeDtypeStruct((n, ATTN_WIDTH), BF16),
            jax.ShapeDtypeStruct((n, ATTN_WIDTH), BF16),
            jax.ShapeDtypeStruct((batch, nblk, ATTN_WIDTH, MOBA_BLOCK), BF16),
            jax.ShapeDtypeStruct((n, conv_ch), BF16),
            jax.ShapeDtypeStruct((steps, blk_per_tile, ATTN_WIDTH), F32),
        ],
        compiler_params=pltpu.CompilerParams(
            dimension_semantics=("parallel",), vmem_limit_bytes=VMEM_LIMIT),
        name="inproj",
    )(x2, norm_g, w_in_bf, wvt_bf)


def _attn_kernel(far_ref, q_ref, k_ref, vt_ref, km_ref, bown_ref, bprev_ref, o_ref,
                 qm_ref, mask_ref, m_ref, l_ref, acc_ref):
    i = pl.program_id(1)
    nblk = k_ref.shape[0]
    tq = q_ref.shape[0]
    n_sel = jnp.minimum(i, MOBA_TOPK)
    i_prev = jnp.maximum(i - 1, 0)
    blk_row = lax.broadcasted_iota(jnp.int32, (nblk, tq), 0)
    lane = lax.broadcasted_iota(jnp.int32, (tq, LANES), 1)
    heads = range(ATTN_HEADS)
    slab_of = lambda h: slice((h // HEADS_PER_SLAB) * LANES, (h // HEADS_PER_SLAB + 1) * LANES)
    rows_of = lambda h: slice(h * HEAD_DIM, (h + 1) * HEAD_DIM)

    def attend(s_t, bias, row, state, vt_blk):
        for c in range(s_t.shape[0] // KEY_CHUNK):
            ks = slice(c * KEY_CHUNK, (c + 1) * KEY_CHUNK)
            s = s_t[ks] if bias is None else s_t[ks] + bias[ks, :]
            peak = jnp.max(s, axis=0, keepdims=True)
            if row is not None:
                peak = peak + row
            m_new = peak if state is None else jnp.maximum(state[0], peak)
            p_t = jnp.exp(s - (m_new if row is None else m_new - row))
            p_sum = jnp.sum(p_t, axis=0, keepdims=True)
            pv = _dot(vt_blk[:, ks], p_t.astype(BF16))
            if state is None:
                state = (m_new, p_sum, pv)
            else:
                alpha = jnp.exp(state[0] - m_new)
                state = (m_new, alpha * state[1] + p_sum, alpha * state[2] + pv)
        return state

    def pipelined(units, scores, consume):
        ready = [scores(u) for u in units[:QK_LOOKAHEAD]]
        for n, u in enumerate(units):
            if n + QK_LOOKAHEAD < len(units):
                ready.append(scores(units[n + QK_LOOKAHEAD]))
            consume(u, ready.pop(0))

    qm = []
    for h in heads:
        hh = h % HEADS_PER_SLAB
        q_slab = q_ref[:, slab_of(h)]
        in_head = (lane >= hh * HEAD_DIM) & (lane < (hh + 1) * HEAD_DIM)
        qm.append(jnp.where(in_head, q_slab, jnp.zeros_like(q_slab)))
        qm_ref[h] = qm[h]
    km_split = []
    for p in range(ATTN_HEADS // HEADS_PER_SLAB):
        km_slab = km_ref[:, p * LANES:(p + 1) * LANES]
        km_hi = km_slab.astype(BF16)
        km_lo = (km_slab - km_hi.astype(F32)).astype(BF16)
        km_split.append(jnp.concatenate([km_hi, km_lo], axis=0))

    def near_scores(u):
        h, own = u
        if own:
            lhs = jnp.concatenate([k_ref[i, :, slab_of(h)], km_split[h // HEADS_PER_SLAB]], axis=0)
        else:
            lhs = k_ref[i_prev, :, slab_of(h)]
        return _dot_nt(lhs, qm[h])

    state = {}

    def near_consume(u, s_all):
        h, own = u
        if own:
            gate = s_all[tq:tq + nblk] + s_all[tq + nblk:tq + 2 * nblk]
            gate = jnp.where(blk_row < i, gate, NEG_INF)
            rank = jnp.zeros((nblk, tq), jnp.int32)
            for r in range(nblk):
                g_r = gate[r:r + 1, :]
                beats = (g_r > gate) | ((g_r == gate) & (blk_row > r))
                rank = rank + beats.astype(jnp.int32)
            sel = (rank < n_sel) & (blk_row < i)
            row_t = jnp.where(sel, jnp.where(blk_row == i - 1, 0.0, far_ref[h]), NEG_INF).astype(F32)
            for r in range(nblk):
                mask_ref[h, r] = jnp.broadcast_to(row_t[r:r + 1, :], (SUBLANES, tq))
            state[h] = attend(s_all[0:tq], bown_ref.at[h], None, None, vt_ref.at[i, rows_of(h)])
        else:
            m, l, acc = attend(s_all, bprev_ref.at[h], mask_ref[h, i_prev][0:1, :], state[h],
                               vt_ref.at[i_prev, rows_of(h)])
            m_ref[h] = m
            l_ref[h] = l
            acc_ref[h] = acc

    pipelined([(h, own) for h in heads for own in (True, False)], near_scores, near_consume)

    def far_block(j, carry):
        def far_consume(h, s_t):
            m, l, acc = attend(s_t, None, mask_ref[h, j][0:1, :], (m_ref[h], l_ref[h], acc_ref[h]),
                               vt_ref.at[j, rows_of(h)])
            m_ref[h] = m
            l_ref[h] = l
            acc_ref[h] = acc

        pipelined(list(heads), lambda h: _dot_nt(k_ref[j, :, slab_of(h)], qm_ref[h]), far_consume)
        return carry

    lax.fori_loop(0, i_prev, far_block, 0)

    for p in range(ATTN_HEADS // HEADS_PER_SLAB):
        pair_t = jnp.concatenate(
            [acc_ref[h] * (1.0 / l_ref[h]) for h in range(p * HEADS_PER_SLAB, (p + 1) * HEADS_PER_SLAB)],
            axis=0)
        o_ref[:, p * LANES:(p + 1) * LANES] = pair_t.T.astype(o_ref.dtype)


def _attention(q, k4, vt4, km3, bown, bprev, far):
    n = q.shape[0]
    batch, nblk = k4.shape[0], k4.shape[1]
    tq = MOBA_BLOCK
    return pl.pallas_call(
        _attn_kernel,
        grid=(batch, nblk),
        in_specs=[
            pl.BlockSpec(memory_space=pltpu.SMEM),
            pl.BlockSpec((tq, ATTN_WIDTH), lambda b, i: (b * nblk + i, 0)),
            pl.BlockSpec((None,) + k4.shape[1:], lambda b, i: (b, 0, 0, 0)),
            pl.BlockSpec((None,) + vt4.shape[1:], lambda b, i: (b, 0, 0, 0)),
            pl.BlockSpec((None,) + km3.shape[1:], lambda b, i: (b, 0, 0)),
            pl.BlockSpec(bown.shape, lambda b, i: (0, 0, 0)),
            pl.BlockSpec(bprev.shape, lambda b, i: (0, 0, 0)),
        ],
        out_specs=pl.BlockSpec((tq, ATTN_WIDTH), lambda b, i: (b * nblk + i, 0)),
        out_shape=jax.ShapeDtypeStruct((n, ATTN_WIDTH), BF16),
        scratch_shapes=[
            pltpu.VMEM((ATTN_HEADS, tq, LANES), BF16),
            pltpu.VMEM((ATTN_HEADS, nblk, SUBLANES, tq), F32),
            pltpu.VMEM((ATTN_HEADS, 1, tq), F32),
            pltpu.VMEM((ATTN_HEADS, 1, tq), F32),
            pltpu.VMEM((ATTN_HEADS, HEAD_DIM, tq), F32),
        ],
        compiler_params=pltpu.CompilerParams(
            dimension_semantics=("parallel", "arbitrary"), vmem_limit_bytes=VMEM_LIMIT),
        name="moba_attn",
    )(far, q, k4, vt4, km3, bown, bprev)


def _conv_kernel(prev_ref, cur_ref, w_ref, b_ref, g_ref, beta_ref, o_ref, win_ref):
    c = pl.program_id(1)
    halo = prev_ref[...].astype(F32)
    win_ref[0:CONV_HALO, :] = jnp.where(c > 0, halo, 0.0)
    win_ref[CONV_HALO:, :] = cur_ref[...].astype(F32)
    rows, ch = cur_ref.shape
    for t in range(rows // CONV_SUB):
        acc = jnp.broadcast_to(b_ref[...], (CONV_SUB, ch))
        for j in range(CONV_KERNEL):
            off = CONV_HALO + t * CONV_SUB - (CONV_KERNEL - 1) + j
            acc = acc + win_ref[off:off + CONV_SUB, :] * w_ref[j:j + 1, :]
        mu = jnp.mean(acc, axis=-1, keepdims=True)
        dev = acc - mu
        var = jnp.mean(dev * dev, axis=-1, keepdims=True)
        y = dev * lax.rsqrt(var + LN_EPS) * g_ref[...] + beta_ref[...]
        o_ref[t * CONV_SUB:(t + 1) * CONV_SUB, :] = (y * jax.nn.sigmoid(y)).astype(o_ref.dtype)


def _conformer_conv(u, conv_w, conv_b, ln_g, ln_b, batch, seq):
    n, ch = u.shape
    tiles = seq // CONV_ROWS
    halo_per_tile = CONV_ROWS // CONV_HALO
    halo_per_seq = seq // CONV_HALO
    const = lambda b, c: (0, 0)
    return pl.pallas_call(
        _conv_kernel,
        grid=(batch, tiles),
        in_specs=[
            pl.BlockSpec((CONV_HALO, ch),
                         lambda b, c: (b * halo_per_seq + jnp.maximum(c * halo_per_tile - 1, 0), 0)),
            pl.BlockSpec((CONV_ROWS, ch), lambda b, c: (b * tiles + c, 0)),
            pl.BlockSpec(conv_w.shape, const),
            pl.BlockSpec((1, ch), const),
            pl.BlockSpec((1, ch), const),
            pl.BlockSpec((1, ch), const),
        ],
        out_specs=pl.BlockSpec((CONV_ROWS, ch), lambda b, c: (b * tiles + c, 0)),
        out_shape=jax.ShapeDtypeStruct((n, ch), BF16),
        scratch_shapes=[pltpu.VMEM((CONV_HALO + CONV_ROWS, ch), F32)],
        compiler_params=pltpu.CompilerParams(
            dimension_semantics=("parallel", "arbitrary"), vmem_limit_bytes=VMEM_LIMIT),
        name="conformer_conv",
    )(u, u, conv_w, conv_b, ln_g, ln_b)


def _ffn_kernel(x_ref, attn_ref, conv_ref, wo_ref, g2_ref, wg_ref, wu_ref, wd_ref, gf_ref,
                o_ref, acc_ref, *, apply_final_norm):
    aw = attn_ref.shape[1]
    h1 = x_ref[...] + _dot(attn_ref[...], wo_ref[0:aw, :]) + _dot(conv_ref[...], wo_ref[aw:, :])
    u2 = _rms(h1, g2_ref[...]).astype(BF16)
    acc_ref[...] = jnp.zeros_like(acc_ref)

    def chunk(c, carry):
        gate = _dot(u2, wg_ref[c])
        up = _dot(u2, wu_ref[c])
        act = (gate * jax.nn.sigmoid(gate) * up).astype(BF16)
        acc_ref[...] += _dot(act, wd_ref[c])
        return carry

    lax.fori_loop(0, wg_ref.shape[0], chunk, 0)
    h2 = h1 + acc_ref[...]
    if apply_final_norm:
        h2 = _rms(h2, gf_ref[...])
    o_ref[...] = h2


def _outproj_ffn(x2, attn, conv, wo_bf, g2, wg3, wu3, wd3, gf, apply_final_norm):
    n, d = x2.shape
    resident = pl.Buffered(1)
    row = lambda r: (r, 0)
    const2 = lambda r: (0, 0)
    const3 = lambda r: (0, 0, 0)
    return pl.pallas_call(
        functools.partial(_ffn_kernel, apply_final_norm=apply_final_norm),
        grid=(n // ROW_TILE,),
        in_specs=[
            pl.BlockSpec((ROW_TILE, d), row),
            pl.BlockSpec((ROW_TILE, attn.shape[1]), row),
            pl.BlockSpec((ROW_TILE, conv.shape[1]), row),
            pl.BlockSpec(wo_bf.shape, const2, pipeline_mode=resident),
            pl.BlockSpec((1, d), const2),
            pl.BlockSpec(wg3.shape, const3, pipeline_mode=resident),
            pl.BlockSpec(wu3.shape, const3, pipeline_mode=resident),
            pl.BlockSpec(wd3.shape, const3, pipeline_mode=resident),
            pl.BlockSpec((1, d), const2),
        ],
        out_specs=pl.BlockSpec((ROW_TILE, d), row),
        out_shape=jax.ShapeDtypeStruct((n, d), F32),
        scratch_shapes=[pltpu.VMEM((ROW_TILE, d), F32)],
        compiler_params=pltpu.CompilerParams(
            dimension_semantics=("parallel",), vmem_limit_bytes=VMEM_LIMIT),
        name="outproj_ffn",
    )(x2, attn, conv, wo_bf, g2, wg3, wu3, wd3, gf)


def kernel(x, mix_norm_g, w_in, rel_bias, conv_w, conv_b, conv_ln_g, conv_ln_b, w_out, ffn_norm_g,
           w_gate, w_up, w_down, final_norm_g):
    batch, seq, d = x.shape
    depth = w_in.shape[0]
    conv_ch = conv_w.shape[-1]
    d_ff = w_gate.shape[-1]
    nblk = seq // MOBA_BLOCK
    n_chunks = d_ff // FFN_CHUNK
    assert seq % ROW_TILE == 0 and seq % CONV_ROWS == 0 and d_ff % FFN_CHUNK == 0
    assert w_in.shape[-1] == 3 * ATTN_WIDTH + 2 * conv_ch

    bown, bprev, far = _bias_tiles(rel_bias)
    h = x.reshape(batch * seq, d)
    for l in range(depth):
        w_in_bf = w_in[l].astype(BF16)
        wvt_bf = w_in[l][:, 2 * ATTN_WIDTH:3 * ATTN_WIDTH].T.astype(BF16)
        q, k, vt4, u, km = _inproj(h, mix_norm_g[l][None], w_in_bf, wvt_bf, conv_ch, batch, seq)
        attn = _attention(q, k.reshape(batch, nblk, MOBA_BLOCK, ATTN_WIDTH), vt4,
                          km.reshape(batch, nblk, ATTN_WIDTH), bown, bprev, far)
        conv = _conformer_conv(u, conv_w[l], conv_b[l][None], conv_ln_g[l][None], conv_ln_b[l][None],
                               batch, seq)
        wg3 = w_gate[l].astype(BF16).reshape(d, n_chunks, FFN_CHUNK).transpose(1, 0, 2)
        wu3 = w_up[l].astype(BF16).reshape(d, n_chunks, FFN_CHUNK).transpose(1, 0, 2)
        wd3 = w_down[l].astype(BF16).reshape(n_chunks, FFN_CHUNK, d)
        h = _outproj_ffn(h, attn, conv, w_out[l].astype(BF16), ffn_norm_g[l][None], wg3, wu3, wd3,
                         final_norm_g[None], apply_final_norm=(l == depth - 1))
    return h.reshape(batch, seq, d)
```

```python
import functools
import math

import jax
import jax.numpy as jnp
import numpy as np
from jax import lax
from jax.experimental import pallas as pl
from jax.experimental.pallas import tpu as pltpu

ATTN_HEADS = 8
HEAD_DIM = 64
ATTN_WIDTH = ATTN_HEADS * HEAD_DIM
CONV_KERNEL = 31
MOBA_BLOCK = 256
MOBA_TOPK = 3
NUM_BUCKETS = 32
MAX_DISTANCE = 128
RMS_EPS = 1e-6
LN_EPS = 1e-5
NEG_INF = -1e30
ATTN_SCALE = HEAD_DIM ** -0.5

LANES = 128
SUBLANES = 8
HEADS_PER_SLAB = LANES // HEAD_DIM
ROW_TILE = 512
CONV_ROWS = 512
CONV_HALO = 32
CONV_SUB = 64
FFN_CHUNK = 256
QK_LOOKAHEAD = 4
KEY_CHUNK = 128
VMEM_LIMIT = 56 * 1024 * 1024

F32 = jnp.float32
BF16 = jnp.bfloat16


def _dot(a, b):
    return jnp.dot(a, b, preferred_element_type=F32)


def _dot_nt(a, b):
    return lax.dot_general(a, b, (((1,), (1,)), ((), ())), preferred_element_type=F32)


def _rms(x, g):
    return x * lax.rsqrt(jnp.mean(x * x, axis=-1, keepdims=True) + RMS_EPS) * g


def _t5_bucket_table(max_dist):
    d = np.arange(max_dist, dtype=np.int64)
    max_exact = NUM_BUCKETS // 2
    df = np.maximum(d, 1).astype(np.float32)
    large = max_exact + (np.log(df / np.float32(max_exact)) / np.float32(math.log(MAX_DISTANCE / max_exact))
                         * np.float32(NUM_BUCKETS - max_exact)).astype(np.int32)
    large = np.minimum(large, NUM_BUCKETS - 1)
    return np.where(d < max_exact, d, large).astype(np.int32)


def _bias_tile_kernel(bd_ref, own_ref, prev_ref):
    blk = own_ref.shape[0]
    by_dist = bd_ref[...]

    def toeplitz(w):
        x = jnp.broadcast_to(w, (blk, 2 * blk))
        return pltpu.roll(x, 0, 1, stride=1, stride_axis=0)[:, blk:]

    prev_ref[...] = toeplitz(by_dist)
    causal = jnp.concatenate([jnp.full((1, blk), NEG_INF, F32), by_dist[:, :blk]], axis=1)
    own_ref[...] = toeplitz(causal)


def _bias_tiles(rel_bias):
    blk = MOBA_BLOCK
    heads = rel_bias.shape[1]
    by_dist = rel_bias.astype(F32).T[:, _t5_bucket_table(2 * blk)]
    tile = jax.ShapeDtypeStruct((heads, blk, blk), F32)
    own, prev = pl.pallas_call(
        _bias_tile_kernel,
        grid=(heads,),
        in_specs=[pl.BlockSpec((None, 1, 2 * blk), lambda h: (h, 0, 0))],
        out_specs=[pl.BlockSpec((None, blk, blk), lambda h: (h, 0, 0))] * 2,
        out_shape=[tile, tile],
        compiler_params=pltpu.CompilerParams(dimension_semantics=("parallel",)),
        name="t5_bias_tiles",
    )(by_dist[:, None, :])
    return own, prev, by_dist[:, 2 * blk - 1]


def _inproj_kernel(x_ref, g_ref, w_ref, wvt_ref, q_ref, k_ref, vt_ref, u_ref, km_ref):
    y = _rms(x_ref[...], g_ref[...]).astype(BF16)
    aw = ATTN_WIDTH
    q_ref[...] = (_dot(y, w_ref[:, 0:aw]) * ATTN_SCALE).astype(BF16)
    k = _dot(y, w_ref[:, aw:2 * aw])
    k_ref[...] = k.astype(BF16)
    nblk = k.shape[0] // MOBA_BLOCK
    km_ref[...] = jnp.mean(k.reshape(nblk, MOBA_BLOCK, aw), axis=1)
    vt = _dot_nt(wvt_ref[...], y)
    for n in range(nblk):
        vt_ref[n] = vt[:, n * MOBA_BLOCK:(n + 1) * MOBA_BLOCK].astype(BF16)
    cw = u_ref.shape[1]
    a = _dot(y, w_ref[:, 3 * aw:3 * aw + cw])
    gate = _dot(y, w_ref[:, 3 * aw + cw:3 * aw + 2 * cw])
    u_ref[...] = (a * jax.nn.sigmoid(gate)).astype(BF16)


def _inproj(x2, norm_g, w_in_bf, wvt_bf, conv_ch, batch, seq):
    n, d = x2.shape
    steps = n // ROW_TILE
    tiles_per_seq = seq // ROW_TILE
    blk_per_tile = ROW_TILE // MOBA_BLOCK
    nblk = seq // MOBA_BLOCK
    const = lambda r: (0, 0)
    return pl.pallas_call(
        _inproj_kernel,
        grid=(steps,),
        in_specs=[
            pl.BlockSpec((ROW_TILE, d), lambda r: (r, 0)),
            pl.BlockSpec((1, d), const),
            pl.BlockSpec(w_in_bf.shape, const),
            pl.BlockSpec(wvt_bf.shape, const),
        ],
        out_specs=[
            pl.BlockSpec((ROW_TILE, ATTN_WIDTH), lambda r: (r, 0)),
            pl.BlockSpec((ROW_TILE, ATTN_WIDTH), lambda r: (r, 0)),
            pl.BlockSpec((None, blk_per_tile, ATTN_WIDTH, MOBA_BLOCK),
                         lambda r: (r // tiles_per_seq, r % tiles_per_seq, 0, 0)),
            pl.BlockSpec((ROW_TILE, conv_ch), lambda r: (r, 0)),
            pl.BlockSpec((None, blk_per_tile, ATTN_WIDTH), lambda r: (r, 0, 0)),
        ],
        out_shape=[
            jax.ShapeDtypeStruct((n, ATTN_WIDTH), BF16),
            jax.ShapeDtypeStruct((n, ATTN_WIDTH), BF16),
            jax.ShapeDtypeStruct((batch, nblk, ATTN_WIDTH, MOBA_BLOCK), BF16),
            jax.ShapeDtypeStruct((n, conv_ch), BF16),
            jax.ShapeDtypeStruct((steps, blk_per_tile, ATTN_WIDTH), F32),
        ],
        compiler_params=pltpu.CompilerParams(
            dimension_semantics=("parallel",), vmem_limit_bytes=VMEM_LIMIT),
        name="inproj",
    )(x2, norm_g, w_in_bf, wvt_bf)


def _attn_kernel(far_ref, q_ref, k_ref, vt_ref, km_ref, bown_ref, bprev_ref, o_ref,
                 qm_ref, mask_ref, m_ref, l_ref, acc_ref):
    i = pl.program_id(1)
    nblk = k_ref.shape[0]
    tq = q_ref.shape[0]
    n_sel = jnp.minimum(i, MOBA_TOPK)
    i_prev = jnp.maximum(i - 1, 0)
    blk_row = lax.broadcasted_iota(jnp.int32, (nblk, tq), 0)
    lane = lax.broadcasted_iota(jnp.int32, (tq, LANES), 1)
    heads = range(ATTN_HEADS)
    slab_of = lambda h: slice((h // HEADS_PER_SLAB) * LANES, (h // HEADS_PER_SLAB + 1) * LANES)
    rows_of = lambda h: slice(h * HEAD_DIM, (h + 1) * HEAD_DIM)

    def attend(s_t, bias, row, state, vt_blk):
        for c in range(s_t.shape[0] // KEY_CHUNK):
            ks = slice(c * KEY_CHUNK, (c + 1) * KEY_CHUNK)
            s = s_t[ks] if bias is None else s_t[ks] + bias[ks, :]
            peak = jnp.max(s, axis=0, keepdims=True)
            if row is not None:
                peak = peak + row
            m_new = peak if state is None else jnp.maximum(state[0], peak)
            p_t = jnp.exp(s - (m_new if row is None else m_new - row))
            p_sum = jnp.sum(p_t, axis=0, keepdims=True)
            pv = _dot(vt_blk[:, ks], p_t.astype(BF16))
            if state is None:
                state = (m_new, p_sum, pv)
            else:
                alpha = jnp.exp(state[0] - m_new)
                state = (m_new, alpha * state[1] + p_sum, alpha * state[2] + pv)
        return state

    def pipelined(units, scores, consume):
        ready = [scores(u) for u in units[:QK_LOOKAHEAD]]
        for n, u in enumerate(units):
            if n + QK_LOOKAHEAD < len(units):
                ready.append(scores(units[n + QK_LOOKAHEAD]))
            consume(u, ready.pop(0))

    qm = []
    for h in heads:
        hh = h % HEADS_PER_SLAB
        q_slab = q_ref[:, slab_of(h)]
        in_head = (lane >= hh * HEAD_DIM) & (lane < (hh + 1) * HEAD_DIM)
        qm.append(jnp.where(in_head, q_slab, jnp.zeros_like(q_slab)))
        qm_ref[h] = qm[h]
    km_split = []
    for p in range(ATTN_HEADS // HEADS_PER_SLAB):
        km_slab = km_ref[:, p * LANES:(p + 1) * LANES]
        km_hi = km_slab.astype(BF16)
        km_lo = (km_slab - km_hi.astype(F32)).astype(BF16)
        km_split.append(jnp.concatenate([km_hi, km_lo], axis=0))

    def near_scores(u):
        h, own = u
        if own:
            lhs = jnp.concatenate([k_ref[i, :, slab_of(h)], km_split[h // HEADS_PER_SLAB]], axis=0)
        else:
            lhs = k_ref[i_prev, :, slab_of(h)]
        return _dot_nt(lhs, qm[h])

    state = {}

    def near_consume(u, s_all):
        h, own = u
        if own:
            gate = s_all[tq:tq + nblk] + s_all[tq + nblk:tq + 2 * nblk]
            gate = jnp.where(blk_row < i, gate, NEG_INF)
            rank = jnp.zeros((nblk, tq), jnp.int32)
            for r in range(nblk):
                g_r = gate[r:r + 1, :]
                beats = (g_r > gate) | ((g_r == gate) & (blk_row > r))
                rank = rank + beats.astype(jnp.int32)
            sel = (rank < n_sel) & (blk_row < i)
            row_t = jnp.where(sel, jnp.where(blk_row == i - 1, 0.0, far_ref[h]), NEG_INF).astype(F32)
            for r in range(nblk):
                mask_ref[h, r] = jnp.broadcast_to(row_t[r:r + 1, :], (SUBLANES, tq))
            state[h] = attend(s_all[0:tq], bown_ref.at[h], None, None, vt_ref.at[i, rows_of(h)])
        else:
            m, l, acc = attend(s_all, bprev_ref.at[h], mask_ref[h, i_prev][0:1, :], state[h],
                               vt_ref.at[i_prev, rows_of(h)])
            m_ref[h] = m
            l_ref[h] = l
            acc_ref[h] = acc

    pipelined([(h, own) for h in heads for own in (True, False)], near_scores, near_consume)

    def far_block(j, carry):
        def far_consume(h, s_t):
            m, l, acc = attend(s_t, None, mask_ref[h, j][0:1, :], (m_ref[h], l_ref[h], acc_ref[h]),
                               vt_ref.at[j, rows_of(h)])
            m_ref[h] = m
            l_ref[h] = l
            acc_ref[h] = acc

        pipelined(list(heads), lambda h: _dot_nt(k_ref[j, :, slab_of(h)], qm_ref[h]), far_consume)
        return carry

    lax.fori_loop(0, i_prev, far_block, 0)

    for p in range(ATTN_HEADS // HEADS_PER_SLAB):
        pair_t = jnp.concatenate(
            [acc_ref[h] * (1.0 / l_ref[h]) for h in range(p * HEADS_PER_SLAB, (p + 1) * HEADS_PER_SLAB)],
            axis=0)
        o_ref[:, p * LANES:(p + 1) * LANES] = pair_t.T.astype(o_ref.dtype)


def _attention(q, k4, vt4, km3, bown, bprev, far):
    n = q.shape[0]
    batch, nblk = k4.shape[0], k4.shape[1]
    tq = MOBA_BLOCK
    return pl.pallas_call(
        _attn_kernel,
        grid=(batch, nblk),
        in_specs=[
            pl.BlockSpec(memory_space=pltpu.SMEM),
            pl.BlockSpec((tq, ATTN_WIDTH), lambda b, i: (b * nblk + i, 0)),
            pl.BlockSpec((None,) + k4.shape[1:], lambda b, i: (b, 0, 0, 0)),
            pl.BlockSpec((None,) + vt4.shape[1:], lambda b, i: (b, 0, 0, 0)),
            pl.BlockSpec((None,) + km3.shape[1:], lambda b, i: (b, 0, 0)),
            pl.BlockSpec(bown.shape, lambda b, i: (0, 0, 0)),
            pl.BlockSpec(bprev.shape, lambda b, i: (0, 0, 0)),
        ],
        out_specs=pl.BlockSpec((tq, ATTN_WIDTH), lambda b, i: (b * nblk + i, 0)),
        out_shape=jax.ShapeDtypeStruct((n, ATTN_WIDTH), BF16),
        scratch_shapes=[
            pltpu.VMEM((ATTN_HEADS, tq, LANES), BF16),
            pltpu.VMEM((ATTN_HEADS, nblk, SUBLANES, tq), F32),
            pltpu.VMEM((ATTN_HEADS, 1, tq), F32),
            pltpu.VMEM((ATTN_HEADS, 1, tq), F32),
            pltpu.VMEM((ATTN_HEADS, HEAD_DIM, tq), F32),
        ],
        compiler_params=pltpu.CompilerParams(
            dimension_semantics=("parallel", "arbitrary"), vmem_limit_bytes=VMEM_LIMIT),
        name="moba_attn",
    )(far, q, k4, vt4, km3, bown, bprev)


def _conv_kernel(prev_ref, cur_ref, w_ref, b_ref, g_ref, beta_ref, o_ref, win_ref):
    c = pl.program_id(1)
    halo = prev_ref[...].astype(F32)
    win_ref[0, 0:CONV_HALO, :] = jnp.where(c > 0, halo, 0.0)
    win_ref[0, CONV_HALO:, :] = cur_ref[...].astype(F32)
    rows, ch = cur_ref.shape
    span = CONV_HALO + rows - SUBLANES
    for r in range(1, SUBLANES):
        win_ref[r, 0:span, :] = win_ref[0, r:r + span, :]
    for t in range(rows // CONV_SUB):
        acc = jnp.broadcast_to(b_ref[...], (CONV_SUB, ch))
        for j in range(CONV_KERNEL):
            off = CONV_HALO + t * CONV_SUB - (CONV_KERNEL - 1) + j
            r = off % SUBLANES
            acc = acc + win_ref[r, off - r:off - r + CONV_SUB, :] * w_ref[j:j + 1, :]
        mu = jnp.mean(acc, axis=-1, keepdims=True)
        dev = acc - mu
        var = jnp.mean(dev * dev, axis=-1, keepdims=True)
        y = dev * lax.rsqrt(var + LN_EPS) * g_ref[...] + beta_ref[...]
        o_ref[t * CONV_SUB:(t + 1) * CONV_SUB, :] = (y * jax.nn.sigmoid(y)).astype(o_ref.dtype)


def _conformer_conv(u, conv_w, conv_b, ln_g, ln_b, batch, seq):
    n, ch = u.shape
    tiles = seq // CONV_ROWS
    halo_per_tile = CONV_ROWS // CONV_HALO
    halo_per_seq = seq // CONV_HALO
    const = lambda b, c: (0, 0)
    return pl.pallas_call(
        _conv_kernel,
        grid=(batch, tiles),
        in_specs=[
            pl.BlockSpec((CONV_HALO, ch),
                         lambda b, c: (b * halo_per_seq + jnp.maximum(c * halo_per_tile - 1, 0), 0)),
            pl.BlockSpec((CONV_ROWS, ch), lambda b, c: (b * tiles + c, 0)),
            pl.BlockSpec(conv_w.shape, const),
            pl.BlockSpec((1, ch), const),
            pl.BlockSpec((1, ch), const),
            pl.BlockSpec((1, ch), const),
        ],
        out_specs=pl.BlockSpec((CONV_ROWS, ch), lambda b, c: (b * tiles + c, 0)),
        out_shape=jax.ShapeDtypeStruct((n, ch), BF16),
        scratch_shapes=[pltpu.VMEM((SUBLANES, CONV_HALO + CONV_ROWS, ch), F32)],
        compiler_params=pltpu.CompilerParams(
            dimension_semantics=("parallel", "arbitrary"), vmem_limit_bytes=VMEM_LIMIT),
        name="conformer_conv",
    )(u, u, conv_w, conv_b, ln_g, ln_b)


def _ffn_kernel(x_ref, attn_ref, conv_ref, wo_ref, g2_ref, wg_ref, wu_ref, wd_ref, gf_ref,
                o_ref, act_ref, *, apply_final_norm):
    mixed = jnp.concatenate([attn_ref[...], conv_ref[...]], axis=1)
    h1 = x_ref[...] + _dot(mixed, wo_ref[...])
    u2 = _rms(h1, g2_ref[...]).astype(BF16)
    for c in range(wg_ref.shape[1] // FFN_CHUNK):
        cols = slice(c * FFN_CHUNK, (c + 1) * FFN_CHUNK)
        gate = _dot(u2, wg_ref[:, cols])
        up = _dot(u2, wu_ref[:, cols])
        act_ref[:, cols] = (gate * jax.nn.sigmoid(gate) * up).astype(BF16)
    h2 = h1 + _dot(act_ref[...], wd_ref[...])
    if apply_final_norm:
        h2 = _rms(h2, gf_ref[...])
    o_ref[...] = h2


def _outproj_ffn(x2, attn, conv, wo_bf, g2, wg_bf, wu_bf, wd_bf, gf, apply_final_norm):
    n, d = x2.shape
    resident = pl.Buffered(1)
    row = lambda r: (r, 0)
    const = lambda r: (0, 0)
    return pl.pallas_call(
        functools.partial(_ffn_kernel, apply_final_norm=apply_final_norm),
        grid=(n // ROW_TILE,),
        in_specs=[
            pl.BlockSpec((ROW_TILE, d), row),
            pl.BlockSpec((ROW_TILE, attn.shape[1]), row),
            pl.BlockSpec((ROW_TILE, conv.shape[1]), row),
            pl.BlockSpec(wo_bf.shape, const, pipeline_mode=resident),
            pl.BlockSpec((1, d), const),
            pl.BlockSpec(wg_bf.shape, const, pipeline_mode=resident),
            pl.BlockSpec(wu_bf.shape, const, pipeline_mode=resident),
            pl.BlockSpec(wd_bf.shape, const, pipeline_mode=resident),
            pl.BlockSpec((1, d), const),
        ],
        out_specs=pl.BlockSpec((ROW_TILE, d), row),
        out_shape=jax.ShapeDtypeStruct((n, d), F32),
        scratch_shapes=[pltpu.VMEM((ROW_TILE, wg_bf.shape[1]), BF16)],
        compiler_params=pltpu.CompilerParams(
            dimension_semantics=("parallel",), vmem_limit_bytes=VMEM_LIMIT),
        name="outproj_ffn",
    )(x2, attn, conv, wo_bf, g2, wg_bf, wu_bf, wd_bf, gf)


def kernel(x, mix_norm_g, w_in, rel_bias, conv_w, conv_b, conv_ln_g, conv_ln_b, w_out, ffn_norm_g,
           w_gate, w_up, w_down, final_norm_g):
    batch, seq, d = x.shape
    depth = w_in.shape[0]
    conv_ch = conv_w.shape[-1]
    d_ff = w_gate.shape[-1]
    nblk = seq // MOBA_BLOCK
    assert seq % ROW_TILE == 0 and seq % CONV_ROWS == 0 and d_ff % FFN_CHUNK == 0
    assert w_in.shape[-1] == 3 * ATTN_WIDTH + 2 * conv_ch

    bown, bprev, far = _bias_tiles(rel_bias)
    h = x.reshape(batch * seq, d)
    for l in range(depth):
        w_in_bf = w_in[l].astype(BF16)
        wvt_bf = w_in[l][:, 2 * ATTN_WIDTH:3 * ATTN_WIDTH].T.astype(BF16)
        q, k, vt4, u, km = _inproj(h, mix_norm_g[l][None], w_in_bf, wvt_bf, conv_ch, batch, seq)
        attn = _attention(q, k.reshape(batch, nblk, MOBA_BLOCK, ATTN_WIDTH), vt4,
                          km.reshape(batch, nblk, ATTN_WIDTH), bown, bprev, far)
        conv = _conformer_conv(u, conv_w[l], conv_b[l][None], conv_ln_g[l][None], conv_ln_b[l][None],
                               batch, seq)
        h = _outproj_ffn(h, attn, conv, w_out[l].astype(BF16), ffn_norm_g[l][None],
                         w_gate[l].astype(BF16), w_up[l].astype(BF16), w_down[l].astype(BF16),
                         final_norm_g[None], apply_final_norm=(l == depth - 1))
    return h.reshape(batch, seq, d)
```

```python
import functools
import math

import jax
import jax.numpy as jnp
import numpy as np
from jax import lax
from jax.experimental import pallas as pl
from jax.experimental.pallas import tpu as pltpu

ATTN_HEADS = 8
HEAD_DIM = 64
ATTN_WIDTH = ATTN_HEADS * HEAD_DIM
CONV_KERNEL = 31
MOBA_BLOCK = 256
MOBA_TOPK = 3
NUM_BUCKETS = 32
MAX_DISTANCE = 128
RMS_EPS = 1e-6
LN_EPS = 1e-5
NEG_INF = -1e30
ATTN_SCALE = HEAD_DIM ** -0.5

LANES = 128
SUBLANES = 8
HEADS_PER_SLAB = LANES // HEAD_DIM
ROW_TILE = 512
CONV_ROWS = 512
CONV_HALO = 32
CONV_SUB = 64
FFN_CHUNK = 256
QK_LOOKAHEAD = 4
QK_CARRY = 2
KEY_CHUNK = 128
SUM_ROWS = 16
LOG2E = math.log2(math.e)
VMEM_LIMIT = 56 * 1024 * 1024

F32 = jnp.float32
BF16 = jnp.bfloat16


def _dot(a, b):
    return jnp.dot(a, b, preferred_element_type=F32)


def _dot_nt(a, b):
    return lax.dot_general(a, b, (((1,), (1,)), ((), ())), preferred_element_type=F32)


def _rms(x, g):
    return x * lax.rsqrt(jnp.mean(x * x, axis=-1, keepdims=True) + RMS_EPS) * g


def _t5_bucket_table(max_dist):
    d = np.arange(max_dist, dtype=np.int64)
    max_exact = NUM_BUCKETS // 2
    df = np.maximum(d, 1).astype(np.float32)
    large = max_exact + (np.log(df / np.float32(max_exact)) / np.float32(math.log(MAX_DISTANCE / max_exact))
                         * np.float32(NUM_BUCKETS - max_exact)).astype(np.int32)
    large = np.minimum(large, NUM_BUCKETS - 1)
    return np.where(d < max_exact, d, large).astype(np.int32)


def _bias_tile_kernel(bd_ref, own_ref, prev_ref):
    blk = own_ref.shape[0]
    by_dist = bd_ref[...]

    def toeplitz(w):
        x = jnp.broadcast_to(w, (blk, 2 * blk))
        return pltpu.roll(x, 0, 1, stride=1, stride_axis=0)[:, blk:]

    prev_ref[...] = toeplitz(by_dist)
    causal = jnp.concatenate([jnp.full((1, blk), NEG_INF, F32), by_dist[:, :blk]], axis=1)
    own_ref[...] = toeplitz(causal)


def _bias_tiles(rel_bias):
    blk = MOBA_BLOCK
    heads = rel_bias.shape[1]
    by_dist = rel_bias.astype(F32).T[:, _t5_bucket_table(2 * blk)] * LOG2E
    tile = jax.ShapeDtypeStruct((heads, blk, blk), F32)
    own, prev = pl.pallas_call(
        _bias_tile_kernel,
        grid=(heads,),
        in_specs=[pl.BlockSpec((None, 1, 2 * blk), lambda h: (h, 0, 0))],
        out_specs=[pl.BlockSpec((None, blk, blk), lambda h: (h, 0, 0))] * 2,
        out_shape=[tile, tile],
        compiler_params=pltpu.CompilerParams(dimension_semantics=("parallel",)),
        name="t5_bias_tiles",
    )(by_dist[:, None, :])
    return own, prev, by_dist[:, 2 * blk - 1]


def _inproj_kernel(x_ref, g_ref, w_ref, wvt_ref, q_ref, k_ref, vt_ref, u_ref, km_ref):
    y = _rms(x_ref[...], g_ref[...]).astype(BF16)
    aw = ATTN_WIDTH
    q_ref[...] = (_dot(y, w_ref[:, 0:aw]) * (ATTN_SCALE * LOG2E)).astype(BF16)
    k = _dot(y, w_ref[:, aw:2 * aw])
    k_ref[...] = k.astype(BF16)
    nblk = k.shape[0] // MOBA_BLOCK
    km_ref[...] = jnp.mean(k.reshape(nblk, MOBA_BLOCK, aw), axis=1)
    vt = _dot_nt(wvt_ref[...], y)
    for n in range(nblk):
        vt_ref[n] = vt[:, n * MOBA_BLOCK:(n + 1) * MOBA_BLOCK].astype(BF16)
    cw = u_ref.shape[1]
    a = _dot(y, w_ref[:, 3 * aw:3 * aw + cw])
    gate = _dot(y, w_ref[:, 3 * aw + cw:3 * aw + 2 * cw])
    u_ref[...] = (a * jax.nn.sigmoid(gate)).astype(BF16)


def _inproj(x2, norm_g, w_in_bf, wvt_bf, conv_ch, batch, seq):
    n, d = x2.shape
    steps = n // ROW_TILE
    tiles_per_seq = seq // ROW_TILE
    blk_per_tile = ROW_TILE // MOBA_BLOCK
    nblk = seq // MOBA_BLOCK
    const = lambda r: (0, 0)
    return pl.pallas_call(
        _inproj_kernel,
        grid=(steps,),
        in_specs=[
            pl.BlockSpec((ROW_TILE, d), lambda r: (r, 0)),
            pl.BlockSpec((1, d), const),
            pl.BlockSpec(w_in_bf.shape, const),
            pl.BlockSpec(wvt_bf.shape, const),
        ],
        out_specs=[
            pl.BlockSpec((ROW_TILE, ATTN_WIDTH), lambda r: (r, 0)),
            pl.BlockSpec((ROW_TILE, ATTN_WIDTH), lambda r: (r, 0)),
            pl.BlockSpec((None, blk_per_tile, ATTN_WIDTH, MOBA_BLOCK),
                         lambda r: (r // tiles_per_seq, r % tiles_per_seq, 0, 0)),
            pl.BlockSpec((ROW_TILE, conv_ch), lambda r: (r, 0)),
            pl.BlockSpec((None, blk_per_tile, ATTN_WIDTH), lambda r: (r, 0, 0)),
        ],
        out_shape=[
            jax.ShapeDtypeStruct((n, ATTN_WIDTH), BF16),
            jax.ShapeDtypeStruct((n, ATTN_WIDTH), BF16),
            jax.ShapeDtypeStruct((batch, nblk, ATTN_WIDTH, MOBA_BLOCK), BF16),
            jax.ShapeDtypeStruct((n, conv_ch), BF16),
            jax.ShapeDtypeStruct((steps, blk_per_tile, ATTN_WIDTH), F32),
        ],
        compiler_params=pltpu.CompilerParams(
            dimension_semantics=("parallel",), vmem_limit_bytes=VMEM_LIMIT),
        name="inproj",
    )(x2, norm_g, w_in_bf, wvt_bf)


def _attn_kernel(far_ref, q_ref, k_ref, vt_ref, km_ref, bown_ref, bprev_ref, o_ref,
                 qm_ref, mask_ref, m_ref, acc_ref, carry_ref):
    i = pl.program_id(1)
    nblk = k_ref.shape[0]
    tq = q_ref.shape[0]
    n_sel = jnp.minimum(i, MOBA_TOPK)
    i_prev = jnp.maximum(i - 1, 0)
    blk_row = lax.broadcasted_iota(jnp.int32, (nblk, tq), 0)
    lane = lax.broadcasted_iota(jnp.int32, (tq, LANES), 1)
    heads = range(ATTN_HEADS)
    slab_of = lambda h: slice((h // HEADS_PER_SLAB) * LANES, (h // HEADS_PER_SLAB + 1) * LANES)
    rows_of = lambda h: slice(h * HEAD_DIM, (h + 1) * HEAD_DIM)

    ones_rows = jnp.ones((SUM_ROWS, KEY_CHUNK), BF16)

    def attend(s_t, bias, row, state, vt_blk):
        for c in range(s_t.shape[0] // KEY_CHUNK):
            ks = slice(c * KEY_CHUNK, (c + 1) * KEY_CHUNK)
            s = s_t[ks] if bias is None else s_t[ks] + bias[ks, :]
            peak = jnp.max(s, axis=0, keepdims=True)
            if row is not None:
                peak = peak + row
            m_new = peak if state is None else jnp.maximum(state[0], peak)
            p_t = jnp.exp2(s - (m_new if row is None else m_new - row))
            pv = _dot(jnp.concatenate([vt_blk[:, ks], ones_rows], axis=0), p_t.astype(BF16))
            if state is None:
                state = (m_new, pv)
            else:
                state = (m_new, jnp.exp2(state[0] - m_new) * state[1] + pv)
        return state

    def pipelined(units, scores, consume):
        ready = [scores(u) for u in units[:QK_LOOKAHEAD]]
        for n, u in enumerate(units):
            if n + QK_LOOKAHEAD < len(units):
                ready.append(scores(units[n + QK_LOOKAHEAD]))
            consume(u, ready.pop(0))

    qm = []
    for h in heads:
        hh = h % HEADS_PER_SLAB
        q_slab = q_ref[:, slab_of(h)]
        in_head = (lane >= hh * HEAD_DIM) & (lane < (hh + 1) * HEAD_DIM)
        qm.append(jnp.where(in_head, q_slab, jnp.zeros_like(q_slab)))
        qm_ref[h] = qm[h]
    km_split = []
    for p in range(ATTN_HEADS // HEADS_PER_SLAB):
        km_slab = km_ref[:, p * LANES:(p + 1) * LANES]
        km_hi = km_slab.astype(BF16)
        km_lo = (km_slab - km_hi.astype(F32)).astype(BF16)
        km_split.append(jnp.concatenate([km_hi, km_lo], axis=0))

    def near_scores(u):
        h, own = u
        if own is None:
            return far_scores(0, h)
        if own:
            lhs = jnp.concatenate([k_ref[i, :, slab_of(h)], km_split[h // HEADS_PER_SLAB]], axis=0)
        else:
            lhs = k_ref[i_prev, :, slab_of(h)]
        return _dot_nt(lhs, qm[h])

    state = {}

    def far_scores(j, h):
        return _dot_nt(k_ref[j, :, slab_of(h)], qm_ref[h])

    def near_consume(u, s_all):
        h, own = u
        if own is None:
            carry_ref[h] = s_all
        elif own:
            gate = s_all[tq:tq + nblk] + s_all[tq + nblk:tq + 2 * nblk]
            gate = jnp.where(blk_row < i, gate, NEG_INF)
            rank = jnp.zeros((nblk, tq), jnp.int32)
            for r in range(nblk):
                g_r = gate[r:r + 1, :]
                beats = (g_r > gate) | ((g_r == gate) & (blk_row > r))
                rank = rank + beats.astype(jnp.int32)
            sel = (rank < n_sel) & (blk_row < i)
            row_t = jnp.where(sel, jnp.where(blk_row == i - 1, 0.0, far_ref[h]), NEG_INF).astype(F32)
            for r in range(nblk):
                mask_ref[h, r] = jnp.broadcast_to(row_t[r:r + 1, :], (SUBLANES, tq))
            state[h] = attend(s_all[0:tq], bown_ref.at[h], None, None, vt_ref.at[i, rows_of(h)])
        else:
            m_ref[h], acc_ref[h] = attend(s_all, bprev_ref.at[h], mask_ref[h, i_prev][0:1, :], state[h],
                                          vt_ref.at[i_prev, rows_of(h)])

    pipelined([(h, own) for h in heads for own in (True, False)] + [(h, None) for h in range(QK_CARRY)],
              near_scores, near_consume)

    def far_block(j, carry):
        j_next = jnp.minimum(j + 1, nblk - 1)

        def scores(u):
            h, ahead = u
            if ahead:
                return far_scores(j_next, h)
            return carry_ref[h] if h < QK_CARRY else far_scores(j, h)

        def consume(u, s_t):
            h, ahead = u
            if ahead:
                carry_ref[h] = s_t
            else:
                m_ref[h], acc_ref[h] = attend(s_t, None, mask_ref[h, j][0:1, :], (m_ref[h], acc_ref[h]),
                                              vt_ref.at[j, rows_of(h)])

        pipelined([(h, False) for h in heads] + [(h, True) for h in range(QK_CARRY)], scores, consume)
        return carry

    lax.fori_loop(0, i_prev, far_block, 0)

    for p in range(ATTN_HEADS // HEADS_PER_SLAB):
        pair_t = jnp.concatenate(
            [acc_ref[h, 0:HEAD_DIM] * (1.0 / acc_ref[h, HEAD_DIM:HEAD_DIM + 1])
             for h in range(p * HEADS_PER_SLAB, (p + 1) * HEADS_PER_SLAB)], axis=0)
        o_ref[:, p * LANES:(p + 1) * LANES] = pair_t.T.astype(o_ref.dtype)


def _attention(q, k4, vt4, km3, bown, bprev, far):
    n = q.shape[0]
    batch, nblk = k4.shape[0], k4.shape[1]
    tq = MOBA_BLOCK
    return pl.pallas_call(
        _attn_kernel,
        grid=(batch, nblk),
        in_specs=[
            pl.BlockSpec(memory_space=pltpu.SMEM),
            pl.BlockSpec((tq, ATTN_WIDTH), lambda b, i: (b * nblk + i, 0)),
            pl.BlockSpec((None,) + k4.shape[1:], lambda b, i: (b, 0, 0, 0)),
            pl.BlockSpec((None,) + vt4.shape[1:], lambda b, i: (b, 0, 0, 0)),
            pl.BlockSpec((None,) + km3.shape[1:], lambda b, i: (b, 0, 0)),
            pl.BlockSpec(bown.shape, lambda b, i: (0, 0, 0)),
            pl.BlockSpec(bprev.shape, lambda b, i: (0, 0, 0)),
        ],
        out_specs=pl.BlockSpec((tq, ATTN_WIDTH), lambda b, i: (b * nblk + i, 0)),
        out_shape=jax.ShapeDtypeStruct((n, ATTN_WIDTH), BF16),
        scratch_shapes=[
            pltpu.VMEM((ATTN_HEADS, tq, LANES), BF16),
            pltpu.VMEM((ATTN_HEADS, nblk, SUBLANES, tq), F32),
            pltpu.VMEM((ATTN_HEADS, 1, tq), F32),
            pltpu.VMEM((ATTN_HEADS, HEAD_DIM + SUM_ROWS, tq), F32),
            pltpu.VMEM((QK_CARRY, MOBA_BLOCK, tq), F32),
        ],
        compiler_params=pltpu.CompilerParams(
            dimension_semantics=("parallel", "arbitrary"), vmem_limit_bytes=VMEM_LIMIT),
        name="moba_attn",
    )(far, q, k4, vt4, km3, bown, bprev)


def _conv_kernel(prev_ref, cur_ref, w_ref, b_ref, g_ref, beta_ref, o_ref, win_ref):
    c = pl.program_id(1)
    halo = prev_ref[...].astype(F32)
    win_ref[0, 0:CONV_HALO, :] = jnp.where(c > 0, halo, 0.0)
    win_ref[0, CONV_HALO:, :] = cur_ref[...].astype(F32)
    rows, ch = cur_ref.shape
    span = CONV_HALO + rows - SUBLANES
    for r in range(1, SUBLANES):
        win_ref[r, 0:span, :] = win_ref[0, r:r + span, :]
    for t in range(rows // CONV_SUB):
        acc = jnp.broadcast_to(b_ref[...], (CONV_SUB, ch))
        for j in range(CONV_KERNEL):
            off = CONV_HALO + t * CONV_SUB - (CONV_KERNEL - 1) + j
            r = off % SUBLANES
            acc = acc + win_ref[r, off - r:off - r + CONV_SUB, :] * w_ref[j:j + 1, :]
        mu = jnp.mean(acc, axis=-1, keepdims=True)
        dev = acc - mu
        var = jnp.mean(dev * dev, axis=-1, keepdims=True)
        y = dev * lax.rsqrt(var + LN_EPS) * g_ref[...] + beta_ref[...]
        o_ref[t * CONV_SUB:(t + 1) * CONV_SUB, :] = (y * jax.nn.sigmoid(y)).astype(o_ref.dtype)


def _conformer_conv(u, conv_w, conv_b, ln_g, ln_b, batch, seq):
    n, ch = u.shape
    tiles = seq // CONV_ROWS
    halo_per_tile = CONV_ROWS // CONV_HALO
    halo_per_seq = seq // CONV_HALO
    const = lambda b, c: (0, 0)
    return pl.pallas_call(
        _conv_kernel,
        grid=(batch, tiles),
        in_specs=[
            pl.BlockSpec((CONV_HALO, ch),
                         lambda b, c: (b * halo_per_seq + jnp.maximum(c * halo_per_tile - 1, 0), 0)),
            pl.BlockSpec((CONV_ROWS, ch), lambda b, c: (b * tiles + c, 0)),
            pl.BlockSpec(conv_w.shape, const),
            pl.BlockSpec((1, ch), const),
            pl.BlockSpec((1, ch), const),
            pl.BlockSpec((1, ch), const),
        ],
        out_specs=pl.BlockSpec((CONV_ROWS, ch), lambda b, c: (b * tiles + c, 0)),
        out_shape=jax.ShapeDtypeStruct((n, ch), BF16),
        scratch_shapes=[pltpu.VMEM((SUBLANES, CONV_HALO + CONV_ROWS, ch), F32)],
        compiler_params=pltpu.CompilerParams(
            dimension_semantics=("parallel", "arbitrary"), vmem_limit_bytes=VMEM_LIMIT),
        name="conformer_conv",
    )(u, u, conv_w, conv_b, ln_g, ln_b)


def _ffn_kernel(x_ref, attn_ref, conv_ref, wo_ref, g2_ref, wg_ref, wu_ref, wd_ref, gf_ref,
                o_ref, act_ref, *, apply_final_norm):
    mixed = jnp.concatenate([attn_ref[...], conv_ref[...]], axis=1)
    h1 = x_ref[...] + _dot(mixed, wo_ref[...])
    u2 = _rms(h1, g2_ref[...]).astype(BF16)
    for c in range(wg_ref.shape[1] // FFN_CHUNK):
        cols = slice(c * FFN_CHUNK, (c + 1) * FFN_CHUNK)
        gate = _dot(u2, wg_ref[:, cols])
        up = _dot(u2, wu_ref[:, cols])
        act_ref[:, cols] = (gate * jax.nn.sigmoid(gate) * up).astype(BF16)
    h2 = h1 + _dot(act_ref[...], wd_ref[...])
    if apply_final_norm:
        h2 = _rms(h2, gf_ref[...])
    o_ref[...] = h2


def _outproj_ffn(x2, attn, conv, wo_bf, g2, wg_bf, wu_bf, wd_bf, gf, apply_final_norm):
    n, d = x2.shape
    resident = pl.Buffered(1)
    row = lambda r: (r, 0)
    const = lambda r: (0, 0)
    return pl.pallas_call(
        functools.partial(_ffn_kernel, apply_final_norm=apply_final_norm),
        grid=(n // ROW_TILE,),
        in_specs=[
            pl.BlockSpec((ROW_TILE, d), row),
            pl.BlockSpec((ROW_TILE, attn.shape[1]), row),
            pl.BlockSpec((ROW_TILE, conv.shape[1]), row),
            pl.BlockSpec(wo_bf.shape, const, pipeline_mode=resident),
            pl.BlockSpec((1, d), const),
            pl.BlockSpec(wg_bf.shape, const, pipeline_mode=resident),
            pl.BlockSpec(wu_bf.shape, const, pipeline_mode=resident),
            pl.BlockSpec(wd_bf.shape, const, pipeline_mode=resident),
            pl.BlockSpec((1, d), const),
        ],
        out_specs=pl.BlockSpec((ROW_TILE, d), row),
        out_shape=jax.ShapeDtypeStruct((n, d), F32),
        scratch_shapes=[pltpu.VMEM((ROW_TILE, wg_bf.shape[1]), BF16)],
        compiler_params=pltpu.CompilerParams(
            dimension_semantics=("parallel",), vmem_limit_bytes=VMEM_LIMIT),
        name="outproj_ffn",
    )(x2, attn, conv, wo_bf, g2, wg_bf, wu_bf, wd_bf, gf)


def kernel(x, mix_norm_g, w_in, rel_bias, conv_w, conv_b, conv_ln_g, conv_ln_b, w_out, ffn_norm_g,
           w_gate, w_up, w_down, final_norm_g):
    batch, seq, d = x.shape
    depth = w_in.shape[0]
    conv_ch = conv_w.shape[-1]
    d_ff = w_gate.shape[-1]
    nblk = seq // MOBA_BLOCK
    assert seq % ROW_TILE == 0 and seq % CONV_ROWS == 0 and d_ff % FFN_CHUNK == 0
    assert w_in.shape[-1] == 3 * ATTN_WIDTH + 2 * conv_ch

    bown, bprev, far = _bias_tiles(rel_bias)
    h = x.reshape(batch * seq, d)
    for l in range(depth):
        w_in_bf = w_in[l].astype(BF16)
        wvt_bf = w_in[l][:, 2 * ATTN_WIDTH:3 * ATTN_WIDTH].T.astype(BF16)
        q, k, vt4, u, km = _inproj(h, mix_norm_g[l][None], w_in_bf, wvt_bf, conv_ch, batch, seq)
        attn = _attention(q, k.reshape(batch, nblk, MOBA_BLOCK, ATTN_WIDTH), vt4,
                          km.reshape(batch, nblk, ATTN_WIDTH), bown, bprev, far)
        conv = _conformer_conv(u, conv_w[l], conv_b[l][None], conv_ln_g[l][None], conv_ln_b[l][None],
                               batch, seq)
        h = _outproj_ffn(h, attn, conv, w_out[l].astype(BF16), ffn_norm_g[l][None],
                         w_gate[l].astype(BF16), w_up[l].astype(BF16), w_down[l].astype(BF16),
                         final_norm_g[None], apply_final_norm=(l == depth - 1))
    return h.reshape(batch, seq, d)
```
